```python
import math
import jax, jax.numpy as jnp
from jax import lax
import numpy as np

D_MODEL = 1024
BATCH = 16
SEQ = 2048
DEPTH = 4
DEC_BATCH = 8
DEC_SEQ = 64
PAST_LEN = 2048

CHUNK = 64
GM_CHUNK = 128
GM_GROUPS = 4
GM_WIDTH = D_MODEL // 2
GM_HEAD = GM_WIDTH // GM_GROUPS
GLA_HEADS = 4
GLA_KDIM = D_MODEL // 2
GLA_VDIM = D_MODEL
GLA_DK = GLA_KDIM // GLA_HEADS
GLA_DV = GLA_VDIM // GLA_HEADS
GLA_RANK = 16
GLA_TEMP = 16.0
D_FF = 2816
IN_COLS = 2 * GM_WIDTH + 2 * GLA_KDIM + 2 * GLA_VDIM + GLA_RANK + 2 * D_MODEL
ALPHA = (2.0 * DEPTH) ** 0.25
BETA = (8.0 * DEPTH) ** -0.25
EPS = 1e-5

kernel_name = "gmlp_gla_gated_parallel_deepnorm_macaron_stream"


def layer_norm(x, g, b):
    xf = x.astype(jnp.float32)
    mu = jnp.mean(xf, axis=-1, keepdims=True)
    var = jnp.mean(jnp.square(xf - mu), axis=-1, keepdims=True)
    return ((xf - mu) * lax.rsqrt(var + EPS) * g.astype(jnp.float32) + b.astype(jnp.float32)).astype(x.dtype)


def rms_norm(x, g):
    xf = x.astype(jnp.float32)
    return xf * lax.rsqrt(jnp.mean(jnp.square(xf), axis=-1, keepdims=True) + EPS) * g.astype(jnp.float32)


def swiglu(x, w1, w3, w2):
    return (jax.nn.silu(x @ w1) * (x @ w3)) @ w2


def gmlp_spatial(v, ws, bs):
    bn, L, g, dg = v.shape
    c = min(L, GM_CHUNK)
    n = L // c
    mask = jnp.tril(jnp.ones((c, c), dtype=bool))
    w = jnp.where(mask, ws[:, :c, :c], 0.0)
    vc = v.reshape(bn, n, c, g, dg)
    s = jnp.einsum('gts,bnsgd->bntgd', w, vc) + jnp.transpose(bs[:, :c])[None, None, :, :, None]
    return s.reshape(bn, L, g, dg)


def gla_scan(q, k, v, lg, s0, chunk):
    bn, L, h, dk = q.shape
    dv = v.shape[-1]
    n = L // chunk
    mask = jnp.tril(jnp.ones((chunk, chunk), dtype=bool))

    def blocks(a):
        return jnp.moveaxis(a.astype(jnp.float32).reshape(bn, n, chunk, *a.shape[2:]), 1, 0)

    def step(S, xs):
        qc, kc, vc, gc = xs
        b = jnp.cumsum(gc, axis=1)
        qe = qc * jnp.exp(b)
        ke = kc * jnp.exp(-b)
        att = jnp.where(mask, jnp.einsum('bthk,bshk->bhts', qe, ke), 0.0)
        o = jnp.einsum('bhts,bshv->bthv', att, vc) + jnp.einsum('bthk,bhkv->bthv', qe, S)
        bl = b[:, -1]
        S = jnp.exp(bl)[..., None] * S + jnp.einsum('bshk,bshv->bhkv', kc * jnp.exp(bl[:, None] - b), vc)
        return S, o

    S, o = lax.scan(step, s0.astype(jnp.float32), (blocks(q), blocks(k), blocks(v), blocks(lg)))
    return jnp.moveaxis(o, 0, 1).reshape(bn, L, h, dv), S


def mixer(h, w_in, gm_ln_g, gm_ln_b, gm_ws, gm_bs, gla_wa2, gla_ba, gla_norm_g, w_pa, w_pb, w_o, s0):
    bn, L, _ = h.shape
    z = h @ w_in
    sizes = [GM_WIDTH, GM_WIDTH, GLA_KDIM, GLA_KDIM, GLA_VDIM, GLA_VDIM, GLA_RANK, D_MODEL, D_MODEL]
    idx = [int(i) for i in np.cumsum(sizes)[:-1]]
    zu, zv, q, k, vg, gg, glr, ga, gb = jnp.split(z, idx, axis=-1)
    u = jax.nn.gelu(zu, approximate=False)
    vn = layer_norm(jax.nn.gelu(zv, approximate=False), gm_ln_g, gm_ln_b)
    s = gmlp_spatial(vn.reshape(bn, L, GM_GROUPS, GM_HEAD), gm_ws, gm_bs).reshape(bn, L, GM_WIDTH)
    o_a = u * s
    lg = jax.nn.log_sigmoid((glr @ gla_wa2 + gla_ba).astype(jnp.float32)) / GLA_TEMP
    o_b, S = gla_scan(
        q.reshape(bn, L, GLA_HEADS, GLA_DK) * (GLA_DK ** -0.5),
        k.reshape(bn, L, GLA_HEADS, GLA_DK),
        vg.reshape(bn, L, GLA_HEADS, GLA_DV),
        lg.reshape(bn, L, GLA_HEADS, GLA_DK),
        s0, min(L, CHUNK))
    o_b = rms_norm(o_b, gla_norm_g).reshape(bn, L, GLA_VDIM).astype(h.dtype) * jax.nn.silu(gg)
    m = jax.nn.sigmoid(ga) * (o_a @ w_pa) + jax.nn.sigmoid(gb) * (o_b @ w_pb)
    return m @ w_o, S, vn


def setup_inputs(seed: int = 0) -> dict:
    key = jax.random.key(seed)
    ks = jax.random.split(key, 24)
    f32 = jnp.float32

    def nrm(k, shape, scale):
        return jax.random.normal(k, shape, f32) * scale

    return {
        "x_prompt": nrm(ks[0], (BATCH, SEQ, D_MODEL), 1.0),
        "x_sample": nrm(ks[1], (DEC_BATCH, DEC_SEQ, D_MODEL), 1.0),
        "state_gla": nrm(ks[2], (DEPTH, DEC_BATCH, GLA_HEADS, GLA_DK, GLA_DV), 0.3),
        "ln_g": 1.0 + nrm(ks[3], (DEPTH, 3, D_MODEL), 0.02),
        "ln_b": nrm(ks[4], (DEPTH, 3, D_MODEL), 0.02),
        "ffn_w1": nrm(ks[5], (DEPTH, 2, D_MODEL, D_FF), D_MODEL ** -0.5),
        "ffn_w3": nrm(ks[6], (DEPTH, 2, D_MODEL, D_FF), D_MODEL ** -0.5),
        "ffn_w2": nrm(ks[7], (DEPTH, 2, D_FF, D_MODEL), BETA * D_FF ** -0.5),
        "w_in": nrm(ks[8], (DEPTH, D_MODEL, IN_COLS), D_MODEL ** -0.5),
        "gm_ln_g": 1.0 + nrm(ks[9], (DEPTH, GM_WIDTH), 0.02),
        "gm_ln_b": nrm(ks[10], (DEPTH, GM_WIDTH), 0.02),
        "gm_ws": nrm(ks[11], (DEPTH, GM_GROUPS, GM_CHUNK, GM_CHUNK), GM_CHUNK ** -0.5),
        "gm_bs": 1.0 + nrm(ks[12], (DEPTH, GM_GROUPS, GM_CHUNK), 0.02),
        "gla_wa2": nrm(ks[13], (DEPTH, GLA_RANK, GLA_KDIM), GLA_RANK ** -0.5),
        "gla_ba": nrm(ks[14], (DEPTH, GLA_KDIM), 0.1) + 1.0,
        "gla_norm_g": 1.0 + nrm(ks[15], (DEPTH, GLA_DV), 0.02),
        "w_pa": nrm(ks[16], (DEPTH, GM_WIDTH, D_MODEL), GM_WIDTH ** -0.5),
        "w_pb": nrm(ks[17], (DEPTH, GLA_VDIM, D_MODEL), GLA_VDIM ** -0.5),
        "w_o": nrm(ks[18], (DEPTH, D_MODEL, D_MODEL), BETA * D_MODEL ** -0.5),
    }


def layer(x, s0, l, ln_g, ln_b, ffn_w1, ffn_w3, ffn_w2, w_in, gm_ln_g, gm_ln_b, gm_ws, gm_bs,
          gla_wa2, gla_ba, gla_norm_g, w_pa, w_pb, w_o):
    x = layer_norm(ALPHA * x + 0.5 * swiglu(x, ffn_w1[l, 0], ffn_w3[l, 0], ffn_w2[l, 0]), ln_g[l, 0], ln_b[l, 0])
    y, S, vn = mixer(x, w_in[l], gm_ln_g[l], gm_ln_b[l], gm_ws[l], gm_bs[l], gla_wa2[l], gla_ba[l],
                     gla_norm_g[l], w_pa[l], w_pb[l], w_o[l], s0)
    x = layer_norm(ALPHA * x + y, ln_g[l, 1], ln_b[l, 1])
    x = layer_norm(ALPHA * x + 0.5 * swiglu(x, ffn_w1[l, 1], ffn_w3[l, 1], ffn_w2[l, 1]), ln_g[l, 2], ln_b[l, 2])
    return x, S, vn


def reference(x_prompt, x_sample, state_gla, ln_g, ln_b, ffn_w1, ffn_w3, ffn_w2, w_in, gm_ln_g, gm_ln_b,
              gm_ws, gm_bs, gla_wa2, gla_ba, gla_norm_g, w_pa, w_pb, w_o):
    hp = x_prompt
    hs = x_sample
    s_prompt = []
    s_sample = []
    v_sample = []
    s_zero = jnp.zeros((x_prompt.shape[0], GLA_HEADS, GLA_DK, GLA_DV), jnp.float32)
    for l in range(DEPTH):
        hp, Sp, _ = layer(hp, s_zero, l, ln_g, ln_b, ffn_w1, ffn_w3, ffn_w2, w_in, gm_ln_g, gm_ln_b, gm_ws,
                          gm_bs, gla_wa2, gla_ba, gla_norm_g, w_pa, w_pb, w_o)
        hs, Ss, vs = layer(hs, state_gla[l], l, ln_g, ln_b, ffn_w1, ffn_w3, ffn_w2, w_in, gm_ln_g, gm_ln_b,
                           gm_ws, gm_bs, gla_wa2, gla_ba, gla_norm_g, w_pa, w_pb, w_o)
        s_prompt.append(Sp)
        s_sample.append(Ss)
        v_sample.append(vs)
    state_gla_prompt = jnp.stack(s_prompt, axis=0)
    state_gla_sample = jnp.stack(s_sample, axis=0)
    state_gmlp_v_sample = jnp.stack(v_sample, axis=0)
    return (hp, hs, state_gla_prompt, state_gla_sample, state_gmlp_v_sample)
```

```python
import functools

import jax
import jax.numpy as jnp
from jax import lax
from jax.experimental import pallas as pl
from jax.experimental.pallas import tpu as pltpu

D_MODEL = 1024
DEPTH = 4
CHUNK = 64
GM_CHUNK = 128
GM_GROUPS = 4
GM_WIDTH = D_MODEL // 2
GM_HEAD = GM_WIDTH // GM_GROUPS
GLA_HEADS = 4
GLA_KDIM = D_MODEL // 2
GLA_VDIM = D_MODEL
GLA_DK = GLA_KDIM // GLA_HEADS
GLA_DV = GLA_VDIM // GLA_HEADS
GLA_RANK = 16
GLA_TEMP = 16.0
D_FF = 2816
ALPHA = (2.0 * DEPTH) ** 0.25
EPS = 1e-5

LANE = 128
VMEM_LIMIT_BYTES = 56 * 1024 * 1024

OFF_U = 0
OFF_V = OFF_U + GM_WIDTH
OFF_Q = OFF_V + GM_WIDTH
OFF_K = OFF_Q + GLA_KDIM
OFF_VG = OFF_K + GLA_KDIM
OFF_GG = OFF_VG + GLA_VDIM
OFF_GA = OFF_GG + GLA_VDIM
OFF_GB = OFF_GA + D_MODEL
OFF_LR = OFF_GB + D_MODEL
IN_COLS_PACKED = OFF_LR + LANE

FFN_ROWS = 512
FFN_CHUNKS = ((0, 1024), (1024, 1024), (2048, 768))
MIXER_ROWS = 256

F32 = jnp.float32
BF16 = jnp.bfloat16


def _layer_norm(x, g, b):
    mu = jnp.mean(x, axis=-1, keepdims=True)
    xc = x - mu
    var = jnp.mean(xc * xc, axis=-1, keepdims=True)
    return xc * lax.rsqrt(var + EPS) * g + b


def _gelu(x):
    return 0.5 * x * (1.0 + lax.erf(x * (2.0 ** -0.5)))


def _dot(a, b):
    return jnp.dot(a, b, preferred_element_type=F32)


def _dot_nt(a, b):
    return lax.dot_general(a, b, (((1,), (1,)), ((), ())), preferred_element_type=F32)


def _dot_tn(a, b):
    return lax.dot_general(a, b, (((0,), (0,)), ((), ())), preferred_element_type=F32)


def _resident(shape):
    zeros = (0,) * len(shape)
    return pl.BlockSpec(shape, lambda *_: zeros, pipeline_mode=pl.Buffered(1))


def _ffn_kernel(x_ref, w1_ref, w3_ref, w2_ref, g_ref, b_ref, o_ref):
    x = x_ref[...]
    xb = x.astype(BF16)
    acc = None
    for lo, n in FFN_CHUNKS:
        h1 = _dot(xb, w1_ref[:, lo:lo + n])
        h3 = _dot(xb, w3_ref[:, lo:lo + n])
        hid = (h1 * jax.nn.sigmoid(h1) * h3).astype(BF16)
        y = _dot(hid, w2_ref[lo:lo + n, :])
        acc = y if acc is None else acc + y
    o_ref[...] = _layer_norm(ALPHA * x + 0.5 * acc, g_ref[...], b_ref[...])


def _ffn(x, w1, w3, w2, g, b):
    n = x.shape[0]
    rows = min(FFN_ROWS, n)
    assert n % rows == 0
    return pl.pallas_call(
        _ffn_kernel,
        grid=(n // rows,),
        in_specs=[
            pl.BlockSpec((rows, D_MODEL), lambda i: (i, 0)),
            _resident((D_MODEL, D_FF)),
            _resident((D_MODEL, D_FF)),
            _resident((D_FF, D_MODEL)),
            _resident((1, D_MODEL)),
            _resident((1, D_MODEL)),
        ],
        out_specs=pl.BlockSpec((rows, D_MODEL), lambda i: (i, 0)),
        out_shape=jax.ShapeDtypeStruct((n, D_MODEL), F32),
        compiler_params=pltpu.CompilerParams(
            dimension_semantics=("parallel",), vmem_limit_bytes=VMEM_LIMIT_BYTES),
        name="ffn_ln",
    )(x, w1, w3, w2, g, b)


def _mixer_kernel(*refs, rows, gm_c, has_s0, emit_vn):
    refs = list(refs)
    h_ref = refs.pop(0)
    s0_ref = refs.pop(0) if has_s0 else None
    (w_in_ref, gmg_ref, gmb_ref, ws_ref, bs_ref, wa2_ref, ba_ref, ng_ref,
     wpa_ref, wpb_ref, wo_ref, lng_ref, lnb_ref) = refs[:13]
    refs = refs[13:]
    out_ref = refs.pop(0)
    sout_ref = refs.pop(0)
    vn_ref = refs.pop(0) if emit_vn else None
    st_ref = refs.pop(0)

    t = pl.program_id(1)
    n_gla = rows // CHUNK

    @pl.when(t == 0)
    def _():
        if has_s0:
            for hd in range(GLA_HEADS):
                st_ref[hd] = s0_ref[0, hd].T
        else:
            st_ref[...] = jnp.zeros_like(st_ref)

    h = h_ref[0]
    hb = h.astype(BF16)

    def proj(lo, n):
        return _dot(hb, w_in_ref[:, lo:lo + n])

    u = _gelu(proj(OFF_U, GM_WIDTH))
    vn = _layer_norm(_gelu(proj(OFF_V, GM_WIDTH)), gmg_ref[...], gmb_ref[...])
    if emit_vn:
        vn_ref[0] = vn
    vnb = vn.astype(BF16)
    r_gm = lax.broadcasted_iota(jnp.int32, (gm_c, gm_c), 0)
    c_gm = lax.broadcasted_iota(jnp.int32, (gm_c, gm_c), 1)
    tril_gm = c_gm <= r_gm
    s_groups = []
    for g in range(GM_GROUPS):
        wg = jnp.where(tril_gm, ws_ref[g, :gm_c, :gm_c], 0.0).astype(BF16)
        bias = bs_ref[:gm_c, g * GM_HEAD:(g + 1) * GM_HEAD]
        parts = [
            _dot(wg, vnb[j * gm_c:(j + 1) * gm_c, g * GM_HEAD:(g + 1) * GM_HEAD]) + bias
            for j in range(rows // gm_c)
        ]
        s_groups.append(parts[0] if len(parts) == 1 else jnp.concatenate(parts, axis=0))
    o_a = u * jnp.concatenate(s_groups, axis=1)
    pa = _dot(o_a.astype(BF16), wpa_ref[...])

    q = proj(OFF_Q, GLA_KDIM) * (GLA_DK ** -0.5)
    k = proj(OFF_K, GLA_KDIM)
    vb = proj(OFF_VG, GLA_VDIM).astype(BF16)
    glr = proj(OFF_LR, LANE).astype(BF16)
    pre = _dot(glr, wa2_ref[...]) + ba_ref[...]
    lg = (jnp.minimum(pre, 0.0) - jnp.log1p(jnp.exp(-jnp.abs(pre)))) * (1.0 / GLA_TEMP)

    r_t = lax.broadcasted_iota(jnp.int32, (rows, rows), 0)
    c_t = lax.broadcasted_iota(jnp.int32, (rows, rows), 1)
    causal = (c_t <= r_t) & ((r_t // CHUNK) == (c_t // CHUNK))
    tri = jnp.where(causal, 1.0, 0.0).astype(BF16)
    lg_hi = lg.astype(BF16)
    rem = lg - lg_hi.astype(F32)
    lg_mid = rem.astype(BF16)
    lg_lo = (rem - lg_mid.astype(F32)).astype(BF16)
    b = _dot(tri, lg_hi) + _dot(tri, lg_mid) + _dot(tri, lg_lo)

    bl_rows = [b[(j + 1) * CHUNK - 1:(j + 1) * CHUNK, :] for j in range(n_gla)]
    bl_full = bl_rows[0] if n_gla == 1 else jnp.concatenate(
        [jnp.broadcast_to(r, (CHUNK, GLA_KDIM)) for r in bl_rows], axis=0)
    qe = (q * jnp.exp(b)).astype(BF16)
    ke = (k * jnp.exp(-b)).astype(BF16)
    kd = (k * jnp.exp(bl_full - b)).astype(BF16)
    decay_rows = [jnp.exp(r) for r in bl_rows]

    ng = ng_ref[...]
    ob_heads = []
    for hd in range(GLA_HEADS):
        ksl = slice(hd * GLA_DK, (hd + 1) * GLA_DK)
        vsl = slice(hd * GLA_DV, (hd + 1) * GLA_DV)
        qh = qe[:, ksl]
        vh = vb[:, vsl]
        att = jnp.where(causal, _dot_nt(qh, ke[:, ksl]), 0.0).astype(BF16)
        o_intra = _dot(att, vh)
        s_t = st_ref[hd]
        inter = []
        for j in range(n_gla):
            rsl = slice(j * CHUNK, (j + 1) * CHUNK)
            inter.append(_dot_nt(qh[rsl], s_t.astype(BF16)))
            s_t = decay_rows[j][:, ksl] * s_t + _dot_tn(vh[rsl], kd[rsl, ksl])
        st_ref[hd] = s_t
        o = o_intra + (inter[0] if n_gla == 1 else jnp.concatenate(inter, axis=0))
        ob_heads.append(o * lax.rsqrt(jnp.mean(o * o, axis=-1, keepdims=True) + EPS) * ng)
    gg = proj(OFF_GG, GLA_VDIM)
    o_b = jnp.concatenate(ob_heads, axis=1) * (gg * jax.nn.sigmoid(gg))
    pb = _dot(o_b.astype(BF16), wpb_ref[...])

    m = jax.nn.sigmoid(proj(OFF_GA, D_MODEL)) * pa + jax.nn.sigmoid(proj(OFF_GB, D_MODEL)) * pb
    y = _dot(m.astype(BF16), wo_ref[...])
    out_ref[0] = _layer_norm(ALPHA * h + y, lng_ref[...], lnb_ref[...])

    @pl.when(t == pl.num_programs(1) - 1)
    def _():
        for hd in range(GLA_HEADS):
            sout_ref[0, hd] = st_ref[hd].T


def _mixer(h, s0, w, lng, lnb, emit_vn):
    bn, seq, _ = h.shape
    rows = min(MIXER_ROWS, seq)
    gm_c = min(seq, GM_CHUNK)
    assert seq % rows == 0 and rows % gm_c == 0 and rows % CHUNK == 0
    has_s0 = s0 is not None
    state_spec = pl.BlockSpec((1, GLA_HEADS, GLA_DK, GLA_DV), lambda i, j: (i, 0, 0, 0))
    in_specs = [pl.BlockSpec((1, rows, D_MODEL), lambda i, j: (i, j, 0))]
    args = [h]
    if has_s0:
        in_specs.append(state_spec)
        args.append(s0)
    weights = [w["w_in"], w["gm_ln_g"], w["gm_ln_b"], w["gm_ws"], w["gm_bs"], w["gla_wa2"],
               w["gla_ba"], w["gla_norm_g"], w["w_pa"], w["w_pb"], w["w_o"], lng, lnb]
    in_specs += [_resident(a.shape) for a in weights]
    args += weights
    out_specs = [pl.BlockSpec((1, rows, D_MODEL), lambda i, j: (i, j, 0)), state_spec]
    out_shape = [jax.ShapeDtypeStruct((bn, seq, D_MODEL), F32),
                 jax.ShapeDtypeStruct((bn, GLA_HEADS, GLA_DK, GLA_DV), F32)]
    if emit_vn:
        out_specs.append(pl.BlockSpec((1, rows, GM_WIDTH), lambda i, j: (i, j, 0)))
        out_shape.append(jax.ShapeDtypeStruct((bn, seq, GM_WIDTH), F32))
    return pl.pallas_call(
        functools.partial(_mixer_kernel, rows=rows, gm_c=gm_c, has_s0=has_s0, emit_vn=emit_vn),
        grid=(bn, seq // rows),
        in_specs=in_specs,
        out_specs=out_specs,
        out_shape=out_shape,
        scratch_shapes=[pltpu.VMEM((GLA_HEADS, GLA_DV, GLA_DK), F32)],
        compiler_params=pltpu.CompilerParams(
            dimension_semantics=("parallel", "arbitrary"), vmem_limit_bytes=VMEM_LIMIT_BYTES),
        name="mixer_ln",
    )(*args)


def _pack_layer(l, ln_g, ln_b, ffn_w1, ffn_w3, ffn_w2, w_in, gm_ln_g, gm_ln_b, gm_ws, gm_bs,
                gla_wa2, gla_ba, gla_norm_g, w_pa, w_pb, w_o):
    wi = w_in[l]
    lr_lo = 2 * GM_WIDTH + 2 * GLA_KDIM + 2 * GLA_VDIM
    gate_lo = lr_lo + GLA_RANK
    w_in_packed = jnp.concatenate(
        [wi[:, :lr_lo], wi[:, gate_lo:], wi[:, lr_lo:gate_lo],
         jnp.zeros((D_MODEL, LANE - GLA_RANK), F32)], axis=1).astype(BF16)
    wa2 = jnp.concatenate(
        [gla_wa2[l], jnp.zeros((LANE - GLA_RANK, GLA_KDIM), F32)], axis=0).astype(BF16)
    bs = jnp.repeat(jnp.transpose(gm_bs[l]), GM_HEAD, axis=1)
    mixer = dict(
        w_in=w_in_packed, gm_ln_g=gm_ln_g[l][None], gm_ln_b=gm_ln_b[l][None], gm_ws=gm_ws[l],
        gm_bs=bs, gla_wa2=wa2, gla_ba=gla_ba[l][None], gla_norm_g=gla_norm_g[l][None],
        w_pa=w_pa[l].astype(BF16), w_pb=w_pb[l].astype(BF16), w_o=w_o[l].astype(BF16))
    ffn = [(ffn_w1[l, i].astype(BF16), ffn_w3[l, i].astype(BF16), ffn_w2[l, i].astype(BF16))
           for i in range(2)]
    ln = [(ln_g[l, i][None], ln_b[l, i][None]) for i in range(3)]
    return ffn, mixer, ln


def _layer(x, s0, ffn, mixer, ln, emit_vn):
    bn, seq, _ = x.shape
    x = _ffn(x.reshape(bn * seq, D_MODEL), *ffn[0], *ln[0]).reshape(bn, seq, D_MODEL)
    outs = _mixer(x, s0, mixer, *ln[1], emit_vn)
    x = _ffn(outs[0].reshape(bn * seq, D_MODEL), *ffn[1], *ln[2]).reshape(bn, seq, D_MODEL)
    return x, outs[1], (outs[2] if emit_vn else None)


def kernel(x_prompt, x_sample, state_gla, ln_g, ln_b, ffn_w1, ffn_w3, ffn_w2, w_in, gm_ln_g, gm_ln_b,
           gm_ws, gm_bs, gla_wa2, gla_ba, gla_norm_g, w_pa, w_pb, w_o):
    hp, hs = x_prompt, x_sample
    s_prompt, s_sample, v_sample = [], [], []
    for l in range(DEPTH):
        ffn, mixer, ln = _pack_layer(l, ln_g, ln_b, ffn_w1, ffn_w3, ffn_w2, w_in, gm_ln_g, gm_ln_b,
                                     gm_ws, gm_bs, gla_wa2, gla_ba, gla_norm_g, w_pa, w_pb, w_o)
        hp, sp, _ = _layer(hp, None, ffn, mixer, ln, emit_vn=False)
        hs, ss, vs = _layer(hs, state_gla[l], ffn, mixer, ln, emit_vn=True)
        s_prompt.append(sp)
        s_sample.append(ss)
        v_sample.append(vs)
    return (hp, hs, jnp.stack(s_prompt, axis=0), jnp.stack(s_sample, axis=0),
            jnp.stack(v_sample, axis=0))
```

```python
import functools

import jax
import jax.numpy as jnp
from jax import lax
from jax.experimental import pallas as pl
from jax.experimental.pallas import tpu as pltpu

D_MODEL = 1024
DEPTH = 4
CHUNK = 64
GM_CHUNK = 128
GM_GROUPS = 4
GM_WIDTH = D_MODEL // 2
GM_HEAD = GM_WIDTH // GM_GROUPS
GLA_HEADS = 4
GLA_KDIM = D_MODEL // 2
GLA_VDIM = D_MODEL
GLA_DK = GLA_KDIM // GLA_HEADS
GLA_DV = GLA_VDIM // GLA_HEADS
GLA_RANK = 16
GLA_TEMP = 16.0
D_FF = 2816
ALPHA = (2.0 * DEPTH) ** 0.25
EPS = 1e-5

LANE = 128
VMEM_LIMIT_BYTES = 56 * 1024 * 1024

OFF_U = 0
OFF_V = OFF_U + GM_WIDTH
OFF_Q = OFF_V + GM_WIDTH
OFF_K = OFF_Q + GLA_KDIM
OFF_VG = OFF_K + GLA_KDIM
OFF_GG = OFF_VG + GLA_VDIM
OFF_GA = OFF_GG + GLA_VDIM
OFF_GB = OFF_GA + D_MODEL
OFF_LR = OFF_GB + D_MODEL
IN_COLS_PACKED = OFF_LR + LANE

FFN_ROWS = 512
FFN_CHUNKS = ((0, 1024), (1024, 1024), (2048, 768))
MIXER_ROWS = 256

F32 = jnp.float32
BF16 = jnp.bfloat16


def _layer_norm(x, g, b):
    mu = jnp.mean(x, axis=-1, keepdims=True)
    xc = x - mu
    var = jnp.mean(xc * xc, axis=-1, keepdims=True)
    return xc * lax.rsqrt(var + EPS) * g + b


def _gelu(x):
    return 0.5 * x * (1.0 + lax.erf(x * (2.0 ** -0.5)))


def _dot(a, b):
    return jnp.dot(a, b, preferred_element_type=F32)


def _resident(shape):
    zeros = (0,) * len(shape)
    return pl.BlockSpec(shape, lambda *_: zeros, pipeline_mode=pl.Buffered(1))


def _ffn_kernel(x_ref, w1_ref, w3_ref, w2_ref, g_ref, b_ref, o_ref):
    x = x_ref[...]
    xb = x.astype(BF16)
    acc = None
    for lo, n in FFN_CHUNKS:
        h1 = _dot(xb, w1_ref[:, lo:lo + n])
        h3 = _dot(xb, w3_ref[:, lo:lo + n])
        hid = (h1 * jax.nn.sigmoid(h1) * h3).astype(BF16)
        y = _dot(hid, w2_ref[lo:lo + n, :])
        acc = y if acc is None else acc + y
    o_ref[...] = _layer_norm(ALPHA * x + 0.5 * acc, g_ref[...], b_ref[...])


def _ffn(x, w1, w3, w2, g, b):
    n = x.shape[0]
    rows = min(FFN_ROWS, n)
    assert n % rows == 0
    return pl.pallas_call(
        _ffn_kernel,
        grid=(n // rows,),
        in_specs=[
            pl.BlockSpec((rows, D_MODEL), lambda i: (i, 0)),
            _resident((D_MODEL, D_FF)),
            _resident((D_MODEL, D_FF)),
            _resident((D_FF, D_MODEL)),
            _resident((1, D_MODEL)),
            _resident((1, D_MODEL)),
        ],
        out_specs=pl.BlockSpec((rows, D_MODEL), lambda i: (i, 0)),
        out_shape=jax.ShapeDtypeStruct((n, D_MODEL), F32),
        compiler_params=pltpu.CompilerParams(
            dimension_semantics=("parallel",), vmem_limit_bytes=VMEM_LIMIT_BYTES),
        name="ffn_ln",
    )(x, w1, w3, w2, g, b)


def _mixer_kernel(*refs, rows, gm_c, has_s0, emit_vn):
    refs = list(refs)
    h_ref = refs.pop(0)
    s0_ref = refs.pop(0) if has_s0 else None
    (w_in_ref, gmg_ref, gmb_ref, ws_ref, bs_ref, wa2_ref, ba_ref, ng_ref,
     wpa_ref, wpb_ref, wo_ref, lng_ref, lnb_ref) = refs[:13]
    refs = refs[13:]
    out_ref = refs.pop(0)
    sout_ref = refs.pop(0)
    vn_ref = refs.pop(0) if emit_vn else None
    st_ref = refs.pop(0)

    t = pl.program_id(1)
    n_gla = rows // CHUNK

    @pl.when(t == 0)
    def _():
        if has_s0:
            for hd in range(GLA_HEADS):
                st_ref[hd] = s0_ref[0, hd].T
        else:
            st_ref[...] = jnp.zeros_like(st_ref)

    h = h_ref[0]
    hb = h.astype(BF16)
    ksl = [slice(hd * GLA_DK, (hd + 1) * GLA_DK) for hd in range(GLA_HEADS)]
    vsl = [slice(hd * GLA_DV, (hd + 1) * GLA_DV) for hd in range(GLA_HEADS)]
    csl = [slice(j * CHUNK, (j + 1) * CHUNK) for j in range(n_gla)]

    zu = jnp.dot(hb, w_in_ref[:, OFF_U:OFF_U + GM_WIDTH], preferred_element_type=F32)
    zv = jnp.dot(hb, w_in_ref[:, OFF_V:OFF_V + GM_WIDTH], preferred_element_type=F32)
    zlr = jnp.dot(hb, w_in_ref[:, OFF_LR:OFF_LR + LANE], preferred_element_type=F32)
    zq = jnp.dot(hb, w_in_ref[:, OFF_Q:OFF_Q + GLA_KDIM], preferred_element_type=F32)
    zk = jnp.dot(hb, w_in_ref[:, OFF_K:OFF_K + GLA_KDIM], preferred_element_type=F32)

    u = _gelu(zu)
    vn = _layer_norm(_gelu(zv), gmg_ref[...], gmb_ref[...])
    if emit_vn:
        vn_ref[0] = vn
    vnb = vn.astype(BF16)

    pre = jnp.dot(zlr.astype(BF16), wa2_ref[...], preferred_element_type=F32) + ba_ref[...]
    zvg = jnp.dot(hb, w_in_ref[:, OFF_VG:OFF_VG + GLA_VDIM], preferred_element_type=F32)
    lg = (jnp.minimum(pre, 0.0) - jnp.log1p(jnp.exp(-jnp.abs(pre)))) * (1.0 / GLA_TEMP)

    r_gm = lax.broadcasted_iota(jnp.int32, (gm_c, gm_c), 0)
    c_gm = lax.broadcasted_iota(jnp.int32, (gm_c, gm_c), 1)
    tril_gm = c_gm <= r_gm
    s_groups = []
    for g in range(GM_GROUPS):
        wg = jnp.where(tril_gm, ws_ref[g, :gm_c, :gm_c], 0.0).astype(BF16)
        bias = bs_ref[:gm_c, g * GM_HEAD:(g + 1) * GM_HEAD]
        parts = [
            jnp.dot(wg, vnb[j * gm_c:(j + 1) * gm_c, g * GM_HEAD:(g + 1) * GM_HEAD],
                    preferred_element_type=F32) + bias
            for j in range(rows // gm_c)
        ]
        s_groups.append(parts[0] if len(parts) == 1 else jnp.concatenate(parts, axis=0))

    r_t = lax.broadcasted_iota(jnp.int32, (rows, rows), 0)
    c_t = lax.broadcasted_iota(jnp.int32, (rows, rows), 1)
    causal = (c_t <= r_t) & ((r_t // CHUNK) == (c_t // CHUNK))
    tri = jnp.where(causal, 1.0, 0.0).astype(BF16)
    lg_hi = lg.astype(BF16)
    rem = lg - lg_hi.astype(F32)
    lg_mid = rem.astype(BF16)
    lg_lo = (rem - lg_mid.astype(F32)).astype(BF16)
    zgg = jnp.dot(hb, w_in_ref[:, OFF_GG:OFF_GG + GLA_VDIM], preferred_element_type=F32)
    b = (jnp.dot(tri, lg_hi, preferred_element_type=F32)
         + jnp.dot(tri, lg_mid, preferred_element_type=F32)
         + jnp.dot(tri, lg_lo, preferred_element_type=F32))
    o_a = u * jnp.concatenate(s_groups, axis=1)
    zga = jnp.dot(hb, w_in_ref[:, OFF_GA:OFF_GA + D_MODEL], preferred_element_type=F32)

    bl_rows = [b[(j + 1) * CHUNK - 1:(j + 1) * CHUNK, :] for j in range(n_gla)]
    bl_full = bl_rows[0] if n_gla == 1 else jnp.concatenate(
        [jnp.broadcast_to(r, (CHUNK, GLA_KDIM)) for r in bl_rows], axis=0)
    qe = (zq * (GLA_DK ** -0.5) * jnp.exp(b)).astype(BF16)
    ke = (zk * jnp.exp(-b)).astype(BF16)
    kd = (zk * jnp.exp(bl_full - b)).astype(BF16)
    decay_rows = [jnp.exp(r) for r in bl_rows]
    vb = zvg.astype(BF16)
    pa = jnp.dot(o_a.astype(BF16), wpa_ref[...], preferred_element_type=F32)

    att = [
        jnp.where(causal, lax.dot_general(qe[:, ksl[hd]], ke[:, ksl[hd]], (((1,), (1,)), ((), ())),
                                          preferred_element_type=F32), 0.0).astype(BF16)
        for hd in range(GLA_HEADS)
    ]
    zgb = jnp.dot(hb, w_in_ref[:, OFF_GB:OFF_GB + D_MODEL], preferred_element_type=F32)
    o_intra = [
        jnp.dot(att[hd], vb[:, vsl[hd]], preferred_element_type=F32) for hd in range(GLA_HEADS)
    ]
    s_t = [st_ref[hd] for hd in range(GLA_HEADS)]
    inter = [[None] * n_gla for _ in range(GLA_HEADS)]
    for j in range(n_gla):
        for hd in range(GLA_HEADS):
            inter[hd][j] = lax.dot_general(
                qe[csl[j], ksl[hd]], s_t[hd].astype(BF16), (((1,), (1,)), ((), ())),
                preferred_element_type=F32)
            upd = lax.dot_general(
                vb[csl[j], vsl[hd]], kd[csl[j], ksl[hd]], (((0,), (0,)), ((), ())),
                preferred_element_type=F32)
            s_t[hd] = decay_rows[j][:, ksl[hd]] * s_t[hd] + upd
    ng = ng_ref[...]
    ob_heads = []
    for hd in range(GLA_HEADS):
        st_ref[hd] = s_t[hd]
        o = o_intra[hd] + (inter[hd][0] if n_gla == 1 else jnp.concatenate(inter[hd], axis=0))
        ob_heads.append(o * lax.rsqrt(jnp.mean(o * o, axis=-1, keepdims=True) + EPS) * ng)
    o_b = jnp.concatenate(ob_heads, axis=1) * (zgg * jax.nn.sigmoid(zgg))
    pb = jnp.dot(o_b.astype(BF16), wpb_ref[...], preferred_element_type=F32)

    m = jax.nn.sigmoid(zga) * pa + jax.nn.sigmoid(zgb) * pb
    y = jnp.dot(m.astype(BF16), wo_ref[...], preferred_element_type=F32)
    out_ref[0] = _layer_norm(ALPHA * h + y, lng_ref[...], lnb_ref[...])

    @pl.when(t == pl.num_programs(1) - 1)
    def _():
        for hd in range(GLA_HEADS):
            sout_ref[0, hd] = st_ref[hd].T


def _mixer(h, s0, w, lng, lnb, emit_vn):
    bn, seq, _ = h.shape
    rows = min(MIXER_ROWS, seq)
    gm_c = min(seq, GM_CHUNK)
    assert seq % rows == 0 and rows % gm_c == 0 and rows % CHUNK == 0
    has_s0 = s0 is not None
    tile_map = lambda i, j: (i, j, 0)
    state_spec = pl.BlockSpec((1, GLA_HEADS, GLA_DK, GLA_DV), lambda i, j: (i, 0, 0, 0))
    in_specs = [pl.BlockSpec((1, rows, D_MODEL), tile_map)]
    args = [h]
    if has_s0:
        in_specs.append(state_spec)
        args.append(s0)
    weights = [w["w_in"], w["gm_ln_g"], w["gm_ln_b"], w["gm_ws"], w["gm_bs"], w["gla_wa2"],
               w["gla_ba"], w["gla_norm_g"], w["w_pa"], w["w_pb"], w["w_o"], lng, lnb]
    in_specs += [_resident(a.shape) for a in weights]
    args += weights
    out_specs = [pl.BlockSpec((1, rows, D_MODEL), tile_map), state_spec]
    out_shape = [jax.ShapeDtypeStruct((bn, seq, D_MODEL), F32),
                 jax.ShapeDtypeStruct((bn, GLA_HEADS, GLA_DK, GLA_DV), F32)]
    if emit_vn:
        out_specs.append(pl.BlockSpec((1, rows, GM_WIDTH), tile_map))
        out_shape.append(jax.ShapeDtypeStruct((bn, seq, GM_WIDTH), F32))
    return pl.pallas_call(
        functools.partial(_mixer_kernel, rows=rows, gm_c=gm_c, has_s0=has_s0, emit_vn=emit_vn),
        grid=(bn, seq // rows),
        in_specs=in_specs,
        out_specs=out_specs,
        out_shape=out_shape,
        scratch_shapes=[pltpu.VMEM((GLA_HEADS, GLA_DV, GLA_DK), F32)],
        compiler_params=pltpu.CompilerParams(
            dimension_semantics=("parallel", "arbitrary"), vmem_limit_bytes=VMEM_LIMIT_BYTES),
        name="mixer_ln",
    )(*args)


def _pack_layer(l, ln_g, ln_b, ffn_w1, ffn_w3, ffn_w2, w_in, gm_ln_g, gm_ln_b, gm_ws, gm_bs,
                gla_wa2, gla_ba, gla_norm_g, w_pa, w_pb, w_o):
    wi = w_in[l]
    lr_lo = 2 * GM_WIDTH + 2 * GLA_KDIM + 2 * GLA_VDIM
    gate_lo = lr_lo + GLA_RANK
    w_in_packed = jnp.concatenate(
        [wi[:, :lr_lo], wi[:, gate_lo:], wi[:, lr_lo:gate_lo],
         jnp.zeros((D_MODEL, LANE - GLA_RANK), F32)], axis=1).astype(BF16)
    wa2 = jnp.concatenate(
        [gla_wa2[l], jnp.zeros((LANE - GLA_RANK, GLA_KDIM), F32)], axis=0).astype(BF16)
    bs = jnp.repeat(jnp.transpose(gm_bs[l]), GM_HEAD, axis=1)
    mixer = dict(
        w_in=w_in_packed, gm_ln_g=gm_ln_g[l][None], gm_ln_b=gm_ln_b[l][None], gm_ws=gm_ws[l],
        gm_bs=bs, gla_wa2=wa2, gla_ba=gla_ba[l][None], gla_norm_g=gla_norm_g[l][None],
        w_pa=w_pa[l].astype(BF16), w_pb=w_pb[l].astype(BF16), w_o=w_o[l].astype(BF16))
    ffn = [(ffn_w1[l, i].astype(BF16), ffn_w3[l, i].astype(BF16), ffn_w2[l, i].astype(BF16))
           for i in range(2)]
    ln = [(ln_g[l, i][None], ln_b[l, i][None]) for i in range(3)]
    return ffn, mixer, ln


def _layer(x, s0, ffn, mixer, ln, emit_vn):
    bn, seq, _ = x.shape
    x = _ffn(x.reshape(bn * seq, D_MODEL), *ffn[0], *ln[0]).reshape(bn, seq, D_MODEL)
    outs = _mixer(x, s0, mixer, *ln[1], emit_vn)
    x = _ffn(outs[0].reshape(bn * seq, D_MODEL), *ffn[1], *ln[2]).reshape(bn, seq, D_MODEL)
    return x, outs[1], (outs[2] if emit_vn else None)


def kernel(x_prompt, x_sample, state_gla, ln_g, ln_b, ffn_w1, ffn_w3, ffn_w2, w_in, gm_ln_g, gm_ln_b,
           gm_ws, gm_bs, gla_wa2, gla_ba, gla_norm_g, w_pa, w_pb, w_o):
    hp, hs = x_prompt, x_sample
    s_prompt, s_sample, v_sample = [], [], []
    for l in range(DEPTH):
        ffn, mixer, ln = _pack_layer(l, ln_g, ln_b, ffn_w1, ffn_w3, ffn_w2, w_in, gm_ln_g, gm_ln_b,
                                     gm_ws, gm_bs, gla_wa2, gla_ba, gla_norm_g, w_pa, w_pb, w_o)
        hp, sp, _ = _layer(hp, None, ffn, mixer, ln, emit_vn=False)
        hs, ss, vs = _layer(hs, state_gla[l], ffn, mixer, ln, emit_vn=True)
        s_prompt.append(sp)
        s_sample.append(ss)
        v_sample.append(vs)
    return (hp, hs, jnp.stack(s_prompt, axis=0), jnp.stack(s_sample, axis=0),
            jnp.stack(v_sample, axis=0))
```

```python
import functools

import jax
import jax.numpy as jnp
from jax import lax
from jax.experimental import pallas as pl
from jax.experimental.pallas import tpu as pltpu

D_MODEL = 1024
DEPTH = 4
CHUNK = 64
GM_CHUNK = 128
GM_GROUPS = 4
GM_WIDTH = D_MODEL // 2
GM_HEAD = GM_WIDTH // GM_GROUPS
GLA_HEADS = 4
GLA_KDIM = D_MODEL // 2
GLA_VDIM = D_MODEL
GLA_DK = GLA_KDIM // GLA_HEADS
GLA_DV = GLA_VDIM // GLA_HEADS
GLA_RANK = 16
GLA_TEMP = 16.0
D_FF = 2816
ALPHA = (2.0 * DEPTH) ** 0.25
EPS = 1e-5

LANE = 128
VMEM_LIMIT_BYTES = 56 * 1024 * 1024

OFF_U = 0
OFF_V = OFF_U + GM_WIDTH
OFF_Q = OFF_V + GM_WIDTH
OFF_K = OFF_Q + GLA_KDIM
OFF_VG = OFF_K + GLA_KDIM
OFF_GG = OFF_VG + GLA_VDIM
OFF_GA = OFF_GG + GLA_VDIM
OFF_GB = OFF_GA + D_MODEL
OFF_LR = OFF_GB + D_MODEL
IN_COLS_PACKED = OFF_LR + LANE

FFN_ROWS = 512
MIXER_ROWS = 256

F32 = jnp.float32
BF16 = jnp.bfloat16


def _layer_norm(x, g, b):
    mu = jnp.mean(x, axis=-1, keepdims=True)
    xc = x - mu
    var = jnp.mean(xc * xc, axis=-1, keepdims=True)
    return xc * lax.rsqrt(var + EPS) * g + b


def _gelu(x):
    return 0.5 * x * (1.0 + lax.erf(x * (2.0 ** -0.5)))


def _dot(a, b):
    return jnp.dot(a, b, preferred_element_type=F32)


def _layer_block(arr, lead):
    n = len(lead)
    zeros = (0,) * (arr.ndim - n)
    return pl.BlockSpec((None,) * n + arr.shape[n:], lambda *_: (*lead, *zeros),
                        pipeline_mode=pl.Buffered(1))


def _ffn_kernel(x_ref, w1_ref, w3_ref, w2_ref, g_ref, b_ref, o_ref):
    x = x_ref[...]
    xb = x.astype(BF16)
    h1 = _dot(xb, w1_ref[...])
    h3 = _dot(xb, w3_ref[...])
    hid = (h1 * jax.nn.sigmoid(h1) * h3).astype(BF16)
    y = _dot(hid, w2_ref[...])
    o_ref[...] = _layer_norm(ALPHA * x + 0.5 * y, g_ref[...], b_ref[...])


def _ffn(x, p, l, i, ln_i):
    n = x.shape[0]
    rows = min(FFN_ROWS, n)
    assert n % rows == 0
    weights = [(p["ffn_w1"], (l, i)), (p["ffn_w3"], (l, i)), (p["ffn_w2"], (l, i)),
               (p["ln_g"], (l, ln_i)), (p["ln_b"], (l, ln_i))]
    return pl.pallas_call(
        _ffn_kernel,
        grid=(n // rows,),
        in_specs=[pl.BlockSpec((rows, D_MODEL), lambda s: (s, 0))]
        + [_layer_block(a, lead) for a, lead in weights],
        out_specs=pl.BlockSpec((rows, D_MODEL), lambda s: (s, 0)),
        out_shape=jax.ShapeDtypeStruct((n, D_MODEL), F32),
        compiler_params=pltpu.CompilerParams(
            dimension_semantics=("parallel",), vmem_limit_bytes=VMEM_LIMIT_BYTES),
        name="ffn_ln",
    )(x, *[a for a, _ in weights])


def _mixer_kernel(*refs, rows, gm_c, has_s0, emit_vn):
    refs = list(refs)
    h_ref = refs.pop(0)
    s0_ref = refs.pop(0) if has_s0 else None
    (w_in_ref, gmg_ref, gmb_ref, ws_ref, bs_ref, wa2_ref, ba_ref, ng_ref,
     wpa_ref, wpb_ref, wo_ref, lng_ref, lnb_ref) = refs[:13]
    refs = refs[13:]
    out_ref = refs.pop(0)
    sout_ref = refs.pop(0)
    vn_ref = refs.pop(0) if emit_vn else None
    st_ref = refs.pop(0)

    t = pl.program_id(1)
    n_gla = rows // CHUNK

    @pl.when(t == 0)
    def _():
        if has_s0:
            for hd in range(GLA_HEADS):
                st_ref[hd] = s0_ref[0, hd].T
        else:
            st_ref[...] = jnp.zeros_like(st_ref)

    h = h_ref[0]
    hb = h.astype(BF16)
    ksl = [slice(hd * GLA_DK, (hd + 1) * GLA_DK) for hd in range(GLA_HEADS)]
    vsl = [slice(hd * GLA_DV, (hd + 1) * GLA_DV) for hd in range(GLA_HEADS)]
    csl = [slice(j * CHUNK, (j + 1) * CHUNK) for j in range(n_gla)]

    zu = jnp.dot(hb, w_in_ref[:, OFF_U:OFF_U + GM_WIDTH], preferred_element_type=F32)
    zv = jnp.dot(hb, w_in_ref[:, OFF_V:OFF_V + GM_WIDTH], preferred_element_type=F32)
    zlr = jnp.dot(hb, w_in_ref[:, OFF_LR:OFF_LR + LANE], preferred_element_type=F32)
    zq = jnp.dot(hb, w_in_ref[:, OFF_Q:OFF_Q + GLA_KDIM], preferred_element_type=F32)
    zk = jnp.dot(hb, w_in_ref[:, OFF_K:OFF_K + GLA_KDIM], preferred_element_type=F32)

    u = _gelu(zu)
    vn = _layer_norm(_gelu(zv), gmg_ref[...], gmb_ref[...])
    if emit_vn:
        vn_ref[0] = vn
    vnb = vn.astype(BF16)

    pre = jnp.dot(zlr.astype(BF16), wa2_ref[...], preferred_element_type=F32) + ba_ref[...]
    zvg = jnp.dot(hb, w_in_ref[:, OFF_VG:OFF_VG + GLA_VDIM], preferred_element_type=F32)
    lg = (jnp.minimum(pre, 0.0) - jnp.log1p(jnp.exp(-jnp.abs(pre)))) * (1.0 / GLA_TEMP)

    r_gm = lax.broadcasted_iota(jnp.int32, (gm_c, gm_c), 0)
    c_gm = lax.broadcasted_iota(jnp.int32, (gm_c, gm_c), 1)
    tril_gm = c_gm <= r_gm
    s_groups = []
    for g in range(GM_GROUPS):
        wg = jnp.where(tril_gm, ws_ref[g, :gm_c, :gm_c], 0.0).astype(BF16)
        bias = bs_ref[:gm_c, g * GM_HEAD:(g + 1) * GM_HEAD]
        parts = [
            jnp.dot(wg, vnb[j * gm_c:(j + 1) * gm_c, g * GM_HEAD:(g + 1) * GM_HEAD],
                    preferred_element_type=F32) + bias
            for j in range(rows // gm_c)
        ]
        s_groups.append(parts[0] if len(parts) == 1 else jnp.concatenate(parts, axis=0))

    r_t = lax.broadcasted_iota(jnp.int32, (rows, rows), 0)
    c_t = lax.broadcasted_iota(jnp.int32, (rows, rows), 1)
    causal = (c_t <= r_t) & ((r_t // CHUNK) == (c_t // CHUNK))
    tri = jnp.where(causal, 1.0, 0.0).astype(BF16)
    lg_hi = lg.astype(BF16)
    rem = lg - lg_hi.astype(F32)
    lg_mid = rem.astype(BF16)
    lg_lo = (rem - lg_mid.astype(F32)).astype(BF16)
    zgg = jnp.dot(hb, w_in_ref[:, OFF_GG:OFF_GG + GLA_VDIM], preferred_element_type=F32)
    b = (jnp.dot(tri, lg_hi, preferred_element_type=F32)
         + jnp.dot(tri, lg_mid, preferred_element_type=F32)
         + jnp.dot(tri, lg_lo, preferred_element_type=F32))
    o_a = u * jnp.concatenate(s_groups, axis=1)
    zga = jnp.dot(hb, w_in_ref[:, OFF_GA:OFF_GA + D_MODEL], preferred_element_type=F32)

    bl_rows = [b[(j + 1) * CHUNK - 1:(j + 1) * CHUNK, :] for j in range(n_gla)]
    bl_full = bl_rows[0] if n_gla == 1 else jnp.concatenate(
        [jnp.broadcast_to(r, (CHUNK, GLA_KDIM)) for r in bl_rows], axis=0)
    qe = (zq * (GLA_DK ** -0.5) * jnp.exp(b)).astype(BF16)
    ke = (zk * jnp.exp(-b)).astype(BF16)
    kd = (zk * jnp.exp(bl_full - b)).astype(BF16)
    decay_rows = [jnp.exp(r) for r in bl_rows]
    vb = zvg.astype(BF16)
    pa = jnp.dot(o_a.astype(BF16), wpa_ref[...], preferred_element_type=F32)

    att = [
        jnp.where(causal, lax.dot_general(qe[:, ksl[hd]], ke[:, ksl[hd]], (((1,), (1,)), ((), ())),
                                          preferred_element_type=F32), 0.0).astype(BF16)
        for hd in range(GLA_HEADS)
    ]
    gb_cols = D_MODEL // n_gla
    zgb_parts = [jnp.dot(hb, w_in_ref[:, OFF_GB:OFF_GB + gb_cols], preferred_element_type=F32)]
    o_intra = [
        jnp.dot(att[hd], vb[:, vsl[hd]], preferred_element_type=F32) for hd in range(GLA_HEADS)
    ]
    s_t = [st_ref[hd] for hd in range(GLA_HEADS)]
    inter = [[None] * n_gla for _ in range(GLA_HEADS)]
    for j in range(n_gla):
        if j > 0:
            lo = OFF_GB + j * gb_cols
            zgb_parts.append(
                jnp.dot(hb, w_in_ref[:, lo:lo + gb_cols], preferred_element_type=F32))
        for hd in range(GLA_HEADS):
            inter[hd][j] = lax.dot_general(
                qe[csl[j], ksl[hd]], s_t[hd].astype(BF16), (((1,), (1,)), ((), ())),
                preferred_element_type=F32)
            upd = lax.dot_general(
                vb[csl[j], vsl[hd]], kd[csl[j], ksl[hd]], (((0,), (0,)), ((), ())),
                preferred_element_type=F32)
            s_t[hd] = decay_rows[j][:, ksl[hd]] * s_t[hd] + upd
    ng = ng_ref[...]
    ob_heads = []
    for hd in range(GLA_HEADS):
        st_ref[hd] = s_t[hd]
        o = o_intra[hd] + (inter[hd][0] if n_gla == 1 else jnp.concatenate(inter[hd], axis=0))
        ob_heads.append(o * lax.rsqrt(jnp.mean(o * o, axis=-1, keepdims=True) + EPS) * ng)
    o_b = jnp.concatenate(ob_heads, axis=1) * (zgg * jax.nn.sigmoid(zgg))
    pb = jnp.dot(o_b.astype(BF16), wpb_ref[...], preferred_element_type=F32)

    zgb = zgb_parts[0] if n_gla == 1 else jnp.concatenate(zgb_parts, axis=1)
    m = jax.nn.sigmoid(zga) * pa + jax.nn.sigmoid(zgb) * pb
    y = jnp.dot(m.astype(BF16), wo_ref[...], preferred_element_type=F32)
    out_ref[0] = _layer_norm(ALPHA * h + y, lng_ref[...], lnb_ref[...])

    @pl.when(t == pl.num_programs(1) - 1)
    def _():
        for hd in range(GLA_HEADS):
            sout_ref[0, hd] = st_ref[hd].T


def _mixer(h, state, p, l, emit_vn):
    bn, seq, _ = h.shape
    rows = min(MIXER_ROWS, seq)
    gm_c = min(seq, GM_CHUNK)
    assert seq % rows == 0 and rows % gm_c == 0 and rows % CHUNK == 0
    has_s0 = state is not None
    tile_map = lambda i, j: (i, j, 0)
    state_block = (1, GLA_HEADS, GLA_DK, GLA_DV)
    in_specs = [pl.BlockSpec((1, rows, D_MODEL), tile_map)]
    args = [h]
    if has_s0:
        in_specs.append(pl.BlockSpec((None,) + state_block, lambda i, j: (l, i, 0, 0, 0)))
        args.append(state)
    weights = [(p[k], (l,)) for k in ("w_in", "gm_ln_g", "gm_ln_b", "gm_ws", "gm_bs", "gla_wa2",
                                      "gla_ba", "gla_norm_g", "w_pa", "w_pb", "w_o")]
    weights += [(p["ln_g"], (l, 1)), (p["ln_b"], (l, 1))]
    in_specs += [_layer_block(a, lead) for a, lead in weights]
    args += [a for a, _ in weights]
    out_specs = [pl.BlockSpec((1, rows, D_MODEL), tile_map),
                 pl.BlockSpec(state_block, lambda i, j: (i, 0, 0, 0))]
    out_shape = [jax.ShapeDtypeStruct((bn, seq, D_MODEL), F32),
                 jax.ShapeDtypeStruct((bn, GLA_HEADS, GLA_DK, GLA_DV), F32)]
    if emit_vn:
        out_specs.append(pl.BlockSpec((1, rows, GM_WIDTH), tile_map))
        out_shape.append(jax.ShapeDtypeStruct((bn, seq, GM_WIDTH), F32))
    return pl.pallas_call(
        functools.partial(_mixer_kernel, rows=rows, gm_c=gm_c, has_s0=has_s0, emit_vn=emit_vn),
        grid=(bn, seq // rows),
        in_specs=in_specs,
        out_specs=out_specs,
        out_shape=out_shape,
        scratch_shapes=[pltpu.VMEM((GLA_HEADS, GLA_DV, GLA_DK), F32)],
        compiler_params=pltpu.CompilerParams(
            dimension_semantics=("parallel", "arbitrary"), vmem_limit_bytes=VMEM_LIMIT_BYTES),
        name="mixer_ln",
    )(*args)


def _pack_params(ln_g, ln_b, ffn_w1, ffn_w3, ffn_w2, w_in, gm_ln_g, gm_ln_b, gm_ws, gm_bs,
                 gla_wa2, gla_ba, gla_norm_g, w_pa, w_pb, w_o):
    lr_lo = 2 * GM_WIDTH + 2 * GLA_KDIM + 2 * GLA_VDIM
    gate_lo = lr_lo + GLA_RANK
    w_in_packed = jnp.concatenate(
        [w_in[:, :, :lr_lo].astype(BF16), w_in[:, :, gate_lo:].astype(BF16),
         w_in[:, :, lr_lo:gate_lo].astype(BF16),
         jnp.zeros((DEPTH, D_MODEL, LANE - GLA_RANK), BF16)], axis=2)
    wa2 = jnp.concatenate(
        [gla_wa2.astype(BF16), jnp.zeros((DEPTH, LANE - GLA_RANK, GLA_KDIM), BF16)], axis=1)
    bs = jnp.repeat(jnp.swapaxes(gm_bs, 1, 2), GM_HEAD, axis=2)
    row = lambda a: a[..., None, :]
    return dict(
        ffn_w1=ffn_w1.astype(BF16), ffn_w3=ffn_w3.astype(BF16), ffn_w2=ffn_w2.astype(BF16),
        ln_g=row(ln_g), ln_b=row(ln_b),
        w_in=w_in_packed, gm_ln_g=row(gm_ln_g), gm_ln_b=row(gm_ln_b), gm_ws=gm_ws, gm_bs=bs,
        gla_wa2=wa2, gla_ba=row(gla_ba), gla_norm_g=row(gla_norm_g),
        w_pa=w_pa.astype(BF16), w_pb=w_pb.astype(BF16), w_o=w_o.astype(BF16))


def _layer(x, state, p, l, emit_vn):
    bn, seq, _ = x.shape
    x = _ffn(x.reshape(bn * seq, D_MODEL), p, l, 0, 0).reshape(bn, seq, D_MODEL)
    outs = _mixer(x, state, p, l, emit_vn)
    x = _ffn(outs[0].reshape(bn * seq, D_MODEL), p, l, 1, 2).reshape(bn, seq, D_MODEL)
    return x, outs[1], (outs[2] if emit_vn else None)


def kernel(x_prompt, x_sample, state_gla, ln_g, ln_b, ffn_w1, ffn_w3, ffn_w2, w_in, gm_ln_g, gm_ln_b,
           gm_ws, gm_bs, gla_wa2, gla_ba, gla_norm_g, w_pa, w_pb, w_o):
    p = _pack_params(ln_g, ln_b, ffn_w1, ffn_w3, ffn_w2, w_in, gm_ln_g, gm_ln_b, gm_ws, gm_bs,
                     gla_wa2, gla_ba, gla_norm_g, w_pa, w_pb, w_o)
    hp, hs = x_prompt, x_sample
    s_prompt, s_sample, v_sample = [], [], []
    for l in range(DEPTH):
        hp, sp, _ = _layer(hp, None, p, l, emit_vn=False)
        hs, ss, vs = _layer(hs, state_gla, p, l, emit_vn=True)
        s_prompt.append(sp)
        s_sample.append(ss)
        v_sample.append(vs)
    return (hp, hs, jnp.stack(s_prompt, axis=0), jnp.stack(s_sample, axis=0),
            jnp.stack(v_sample, axis=0))
```
